```python
import jax, jax.numpy as jnp
from jax import lax
import numpy as np

D_MODEL = 1024
BATCH = 8
SEQ = 2048
DEPTH = 2
DEC_BATCH = 128
DEC_SEQ = 8
PAST_LEN = 16384
PAGE_SIZE = 128

N_EVEN = (DEPTH + 1) // 2
N_ODD = DEPTH // 2
D_POOL = D_MODEL // 2
POOL_WINDOWS = (2, 4, 8, 16)
N_POOL_GROUPS = len(POOL_WINDOWS)
POOL_GROUP = D_POOL // N_POOL_GROUPS
POOL_BUF = max(POOL_WINDOWS) - 1
D_SCONV = D_MODEL // 2
CONV_W = 3
CONV_BUF = CONV_W - 1
D_PROJ0 = D_POOL + 3 * D_SCONV
D_MIX0 = D_POOL + D_SCONV
D_GATE = D_MODEL
CHUNK = 128
N_SG_HEADS = 8
SG_HEAD = D_GATE // N_SG_HEADS
D_FF = 2816
N_MOD = 6
EPS = 1e-6

kernel_name = "hybrid_pool_sconv_sgmlp_convffn_step"


def rmsnorm(x, g):
    xf = x.astype(jnp.float32)
    y = xf * lax.rsqrt(jnp.mean(xf * xf, axis=-1, keepdims=True) + EPS)
    return (y * g.astype(jnp.float32)).astype(x.dtype)


def causal_dwconv(x, prev, w, b):
    L = x.shape[1]
    ext = jnp.concatenate([prev.astype(x.dtype), x], axis=1)
    y = b
    for k in range(CONV_W):
        y = y + w[k] * ext[:, k:k + L]
    return y, ext[:, -CONV_BUF:]


def pool_mixer(a, prev, start_pos, w_pool, s_pool):
    B, L, _ = a.shape
    ext = jnp.concatenate([prev.astype(a.dtype), a], axis=1)
    cs = jnp.cumsum(ext.astype(jnp.float32), axis=1)
    cs = jnp.concatenate([jnp.zeros((B, 1, D_POOL), jnp.float32), cs], axis=1)
    end = cs[:, POOL_BUF + 1:]
    pos = jnp.arange(L, dtype=jnp.int32) + start_pos
    means = []
    for g, w in enumerate(POOL_WINDOWS):
        sl = slice(g * POOL_GROUP, (g + 1) * POOL_GROUP)
        s = end[..., sl] - cs[:, POOL_BUF + 1 - w:POOL_BUF + 1 - w + L, sl]
        cnt = jnp.minimum(w, pos + 1).astype(jnp.float32)[None, :, None]
        means.append(s / cnt)
    d = (jnp.concatenate(means, axis=-1) - a.astype(jnp.float32)).astype(a.dtype)
    d = d.reshape(B, L, N_POOL_GROUPS, POOL_GROUP)
    y = jnp.einsum('blgc,gcd->blgd', d, w_pool).reshape(B, L, D_POOL) * s_pool
    return y, ext[:, -POOL_BUF:]


def spatial_gating(v, w_s, b_s):
    B, L, _ = v.shape
    n_chunks = -(-L // CHUNK)
    pad = n_chunks * CHUNK - L
    vp = jnp.pad(v, ((0, 0), (0, pad), (0, 0))).reshape(B, n_chunks, CHUNK, N_SG_HEADS, SG_HEAD)
    mask = jnp.tril(jnp.ones((CHUNK, CHUNK), dtype=bool))
    wm = jnp.where(mask[None], w_s, jnp.zeros((), w_s.dtype))
    z = jnp.einsum('hts,bnshd->bnthd', wm, vp) + b_s.T[None, None, :, :, None]
    return z.reshape(B, n_chunks * CHUNK, D_GATE)[:, :L]


def trunk(x, c, start_pos, pool_prev, sconv_prev, ffn_prev,
          norm_mix, norm_ffn, w_ada, b_ada, w_in0, w_pool, s_pool, sconv_w, sconv_b,
          w_out0, w_uv1, g_v1, w_s1, b_s1, w_out1, ffn_up, ffn_conv_w, ffn_conv_b,
          ffn_down, norm_final):
    new_pool, new_sconv, new_ffn, new_v = [], [], [], []
    for l in range(DEPTH):
        mod = jax.nn.silu(c) @ w_ada[l] + b_ada[l]
        sh1, sc1, g1, sh2, sc2, g2 = [m[:, None, :] for m in jnp.split(mod, N_MOD, axis=-1)]
        h = rmsnorm(x, norm_mix[l]) * (1 + sc1) + sh1
        if l % 2 == 0:
            e = l // 2
            proj = h @ w_in0[e]
            a, xin, bg, cg = jnp.split(proj, [D_POOL, D_POOL + D_SCONV, D_POOL + 2 * D_SCONV], axis=-1)
            ya, npool = pool_mixer(a, pool_prev[e], start_pos, w_pool[e], s_pool[e])
            yc, nconv = causal_dwconv(cg * xin, sconv_prev[e], sconv_w[e], sconv_b[e])
            yb = bg * yc
            mix = jnp.concatenate([ya, yb], axis=-1) @ w_out0[e]
            new_pool.append(npool)
            new_sconv.append(nconv)
        else:
            o = l // 2
            uv = jax.nn.gelu(h @ w_uv1[o])
            u, v = jnp.split(uv, 2, axis=-1)
            v = rmsnorm(v, g_v1[o])
            mix = (u * spatial_gating(v, w_s1[o], b_s1[o])) @ w_out1[o]
            new_v.append(v)
        x = x + g1 * mix
        h = rmsnorm(x, norm_ffn[l]) * (1 + sc2) + sh2
        up, nffn = causal_dwconv(h @ ffn_up[l], ffn_prev[l], ffn_conv_w[l], ffn_conv_b[l])
        ga, va = jnp.split(up, 2, axis=-1)
        x = x + g2 * ((jax.nn.gelu(ga) * va) @ ffn_down[l])
        new_ffn.append(nffn)
    y = rmsnorm(x, norm_final)
    return y, jnp.stack(new_pool), jnp.stack(new_sconv), jnp.stack(new_ffn), jnp.stack(new_v)


def setup_inputs(seed: int = 0) -> dict:
    key = jax.random.key(seed)
    ks = jax.random.split(key, 32)
    n = lambda k, s, sc=1.0: jax.random.normal(k, s, jnp.float32) * sc
    return {
        "x_prompt": n(ks[0], (BATCH, SEQ, D_MODEL)),
        "x_sample": n(ks[1], (DEC_BATCH, DEC_SEQ, D_MODEL)),
        "state_pool": n(ks[2], (N_EVEN, DEC_BATCH, POOL_BUF, D_POOL)),
        "state_sconv": n(ks[3], (N_EVEN, DEC_BATCH, CONV_BUF, D_SCONV)),
        "state_ffn": n(ks[4], (DEPTH, DEC_BATCH, CONV_BUF, 2 * D_FF)),
        "c_prompt": n(ks[5], (BATCH, D_MODEL)),
        "c_sample": n(ks[6], (DEC_BATCH, D_MODEL)),
        "norm_mix": 1.0 + n(ks[7], (DEPTH, D_MODEL), 0.02),
        "norm_ffn": 1.0 + n(ks[8], (DEPTH, D_MODEL), 0.02),
        "w_ada": n(ks[9], (DEPTH, D_MODEL, N_MOD * D_MODEL), 0.3 * D_MODEL ** -0.5),
        "b_ada": n(ks[10], (DEPTH, N_MOD * D_MODEL), 0.02),
        "w_in0": n(ks[11], (N_EVEN, D_MODEL, D_PROJ0), D_MODEL ** -0.5),
        "w_pool": n(ks[12], (N_EVEN, N_POOL_GROUPS, POOL_GROUP, POOL_GROUP), POOL_GROUP ** -0.5),
        "s_pool": 1.0 + n(ks[13], (N_EVEN, D_POOL), 0.05),
        "sconv_w": n(ks[14], (N_EVEN, CONV_W, D_SCONV), CONV_W ** -0.5),
        "sconv_b": n(ks[15], (N_EVEN, D_SCONV), 0.02),
        "w_out0": n(ks[16], (N_EVEN, D_MIX0, D_MODEL), D_MIX0 ** -0.5),
        "w_uv1": n(ks[17], (N_ODD, D_MODEL, 2 * D_GATE), D_MODEL ** -0.5),
        "g_v1": 1.0 + n(ks[18], (N_ODD, D_GATE), 0.02),
        "w_s1": n(ks[19], (N_ODD, N_SG_HEADS, CHUNK, CHUNK), CHUNK ** -0.5),
        "b_s1": 1.0 + n(ks[20], (N_ODD, N_SG_HEADS, CHUNK), 0.02),
        "w_out1": n(ks[21], (N_ODD, D_GATE, D_MODEL), D_GATE ** -0.5),
        "ffn_up": n(ks[22], (DEPTH, D_MODEL, 2 * D_FF), D_MODEL ** -0.5),
        "ffn_conv_w": n(ks[23], (DEPTH, CONV_W, 2 * D_FF), CONV_W ** -0.5),
        "ffn_conv_b": n(ks[24], (DEPTH, 2 * D_FF), 0.02),
        "ffn_down": n(ks[25], (DEPTH, D_FF, D_MODEL), D_FF ** -0.5),
        "norm_final": 1.0 + n(ks[26], (D_MODEL,), 0.02),
    }


def reference(x_prompt, x_sample, state_pool, state_sconv, state_ffn, c_prompt, c_sample,
              norm_mix, norm_ffn, w_ada, b_ada, w_in0, w_pool, s_pool, sconv_w, sconv_b,
              w_out0, w_uv1, g_v1, w_s1, b_s1, w_out1, ffn_up, ffn_conv_w, ffn_conv_b,
              ffn_down, norm_final):
    weights = (norm_mix, norm_ffn, w_ada, b_ada, w_in0, w_pool, s_pool, sconv_w, sconv_b,
               w_out0, w_uv1, g_v1, w_s1, b_s1, w_out1, ffn_up, ffn_conv_w, ffn_conv_b,
               ffn_down, norm_final)
    dt = x_prompt.dtype
    pool0 = jnp.zeros((N_EVEN, BATCH, POOL_BUF, D_POOL), dt)
    sconv0 = jnp.zeros((N_EVEN, BATCH, CONV_BUF, D_SCONV), dt)
    ffn0 = jnp.zeros((DEPTH, BATCH, CONV_BUF, 2 * D_FF), dt)
    y_prompt, pool_prompt, sconv_prompt, ffn_prompt, _ = trunk(
        x_prompt, c_prompt, 0, pool0, sconv0, ffn0, *weights)
    y_sample, pool_sample, sconv_sample, ffn_sample, sg_v_sample = trunk(
        x_sample, c_sample, PAST_LEN, state_pool, state_sconv, state_ffn, *weights)
    return (y_prompt, y_sample, pool_prompt, pool_sample, sconv_prompt, sconv_sample,
            ffn_prompt, ffn_sample, sg_v_sample)
```

```python
import functools
import math

import jax
import jax.numpy as jnp
from jax import lax
from jax.experimental import pallas as pl
from jax.experimental.pallas import tpu as pltpu

D_MODEL = 1024
PAST_LEN = 16384
D_POOL = 512
D_SCONV = 512
POOL_WINDOWS = (2, 4, 8, 16)
POOL_GROUP = 128
POOL_BUF = 15
CONV_BUF = 2
CHUNK = 128
N_SG_HEADS = 8
SG_HEAD = 128
D_FF = 2816
N_MOD = 6
EPS = 1e-6

SUBLANES = 8
TILE_ROWS = 512
FF_CHUNK = 256
N_FF_CHUNKS = D_FF // FF_CHUNK
SAMPLE_GROUP = 64
MOD_TILE_N = 1536
VMEM_LIMIT = 60 * 1024 * 1024

F32 = jnp.float32
BF16 = jnp.bfloat16

_GELU_C0 = math.sqrt(2.0 / math.pi)
_GELU_C1 = 0.044715 * _GELU_C0


def _round_up(n, m):
    return -(-n // m) * m


def _gelu(x):
    return 0.5 * x * (1.0 + jnp.tanh(x * (_GELU_C0 + _GELU_C1 * (x * x))))


def _dot(a, b):
    return jnp.dot(a, b, preferred_element_type=F32)


def _rms(x, g):
    ms = jnp.mean(x * x, axis=-1, keepdims=True)
    return x * lax.rsqrt(ms + EPS) * g


def _mod_kernel(c_ref, w_ref, b_ref, o_ref):
    c = c_ref[...]
    s = (c * jax.nn.sigmoid(c)).astype(BF16)
    o_ref[...] = _dot(s, w_ref[...].astype(BF16)) + b_ref[...]


def _modulation(c_all, w_ada, b_ada):
    depth, _, n = w_ada.shape
    rows = c_all.shape[0]
    return pl.pallas_call(
        _mod_kernel,
        grid=(depth, n // MOD_TILE_N),
        in_specs=[
            pl.BlockSpec((rows, D_MODEL), lambda l, j: (0, 0)),
            pl.BlockSpec((None, D_MODEL, MOD_TILE_N), lambda l, j: (l, 0, j)),
            pl.BlockSpec((None, 1, MOD_TILE_N), lambda l, j: (l, 0, j)),
        ],
        out_specs=pl.BlockSpec((None, rows, MOD_TILE_N), lambda l, j: (l, 0, j)),
        out_shape=jax.ShapeDtypeStruct((depth, rows, n), F32),
        compiler_params=pltpu.CompilerParams(
            dimension_semantics=("arbitrary", "arbitrary"), vmem_limit_bytes=VMEM_LIMIT),
        name="adaln_mod",
    )(c_all, w_ada, b_ada.reshape(depth, 1, n))


def _mod_parts(mod_ref):
    m = mod_ref[...]
    return [m[:, k * D_MODEL:(k + 1) * D_MODEL] for k in range(N_MOD)]


def _conv3(buf, head, stride, rows, w_ref, b_ref, cols):
    y = b_ref[:, cols] + w_ref[0:1, cols] * buf[head - 2 * stride:head - 2 * stride + rows, :]
    y = y + w_ref[1:2, cols] * buf[head - stride:head - stride + rows, :]
    return y + w_ref[2:3, cols] * buf[head:head + rows, :]


def _conv_ffn(h_scr, x1_scr, g2, wup_ref, fcw_ref, fcb_ref, wdn_ref, fout_ref,
              ug_scr, uv_scr, act_scr, *, rows, stride):
    head = ug_scr.shape[0] - rows
    hb = h_scr[...]
    for j in range(N_FF_CHUNKS):
        gc = slice(j * FF_CHUNK, (j + 1) * FF_CHUNK)
        vc = slice(D_FF + j * FF_CHUNK, D_FF + (j + 1) * FF_CHUNK)
        ug_scr[0:head, :] = fout_ref[:, gc]
        uv_scr[0:head, :] = fout_ref[:, vc]
        ug_scr[head:, :] = _dot(hb, wup_ref[:, gc])
        uv_scr[head:, :] = _dot(hb, wup_ref[:, vc])
        fout_ref[:, gc] = ug_scr[rows:rows + head, :]
        fout_ref[:, vc] = uv_scr[rows:rows + head, :]
        ga = _conv3(ug_scr, head, stride, rows, fcw_ref, fcb_ref, gc)
        va = _conv3(uv_scr, head, stride, rows, fcw_ref, fcb_ref, vc)
        act_scr[:, gc] = (_gelu(ga) * va).astype(BF16)
    return _dot(act_scr[...], wdn_ref[...])


def _layer_kernel(*refs, kind, n_slabs, slab, stride, pos0, final_norm, emit_v):
    rows = n_slabs * slab
    it = iter(refs)
    x_ref, mod_ref = next(it), next(it)
    if kind == "even":
        pst_ref, cst_ref = next(it), next(it)
    fst_ref = next(it)
    nm_ref, nf_ref = next(it), next(it)
    if kind == "even":
        win_ref, wpool_ref, spool_ref, scw_ref, scb_ref, wout_ref = (next(it) for _ in range(6))
    else:
        wuv_ref, gv_ref, ws_ref, btab_ref, wout_ref = (next(it) for _ in range(5))
    wup_ref, fcw_ref, fcb_ref, wdn_ref = (next(it) for _ in range(4))
    nfin_ref = next(it) if final_norm else None
    y_ref = next(it)
    if kind == "even":
        pout_ref, cout_ref = next(it), next(it)
    fout_ref = next(it)
    vout_ref = next(it) if emit_v else None
    h_scr, x1_scr, mix_scr, ug_scr, uv_scr, act_scr = (next(it) for _ in range(6))
    if kind == "even":
        pbuf, zbuf, bg_scr = next(it), next(it), next(it)
    else:
        u_scr, v_scr = next(it), next(it)

    i = pl.program_id(1)

    @pl.when(i == 0)
    def _():
        fout_ref[...] = fst_ref[...]
        if kind == "even":
            pout_ref[...] = pst_ref[...]
            cout_ref[...] = cst_ref[...]

    sh1, sc1, g1, sh2, sc2, g2 = _mod_parts(mod_ref)

    for t in range(n_slabs):
        r = slice(t * slab, (t + 1) * slab)
        h = _rms(x_ref[r, :], nm_ref[...]) * (1.0 + sc1) + sh1
        h_scr[r, :] = h.astype(BF16)
    hb = h_scr[...]

    if kind == "even":
        hp = pbuf.shape[0] - rows
        hc = zbuf.shape[0] - rows
        pbuf[0:hp, :] = pout_ref[...]
        pbuf[hp:, :] = _dot(hb, win_ref[:, 0:D_POOL])
        zbuf[0:hc, :] = cout_ref[...]
        xin = _dot(hb, win_ref[:, D_POOL:D_POOL + D_SCONV])
        cg = _dot(hb, win_ref[:, D_POOL + 2 * D_SCONV:D_POOL + 3 * D_SCONV])
        zbuf[hc:, :] = cg * xin
        bg_scr[...] = _dot(hb, win_ref[:, D_POOL + D_SCONV:D_POOL + 2 * D_SCONV])

        ridx = lax.broadcasted_iota(jnp.int32, (rows, POOL_GROUP), 0)
        pos = pos0 + (i * rows + ridx) // stride
        for g, w in enumerate(POOL_WINDOWS):
            cols = slice(g * POOL_GROUP, (g + 1) * POOL_GROUP)
            a_g = pbuf[hp:hp + rows, cols]
            s = a_g
            for j in range(1, w):
                s = s + pbuf[hp - j * stride:hp - j * stride + rows, cols]
            cnt = jnp.minimum(w, pos + 1).astype(F32)
            d = (s / cnt - a_g).astype(BF16)
            ya = _dot(d, wpool_ref[g]) * spool_ref[:, cols]
            mix_scr[:, cols] = ya.astype(BF16)
        pout_ref[...] = pbuf[rows:rows + hp, :]

        yc = _conv3(zbuf, hc, stride, rows, scw_ref, scb_ref, slice(0, D_SCONV))
        mix_scr[:, D_POOL:D_POOL + D_SCONV] = (bg_scr[...] * yc).astype(BF16)
        cout_ref[...] = zbuf[rows:rows + hc, :]
    else:
        u_scr[...] = _gelu(_dot(hb, wuv_ref[:, 0:D_MODEL]))
        v = _gelu(_dot(hb, wuv_ref[:, D_MODEL:2 * D_MODEL]))
        v = _rms(v, gv_ref[...])
        v_scr[...] = v
        if emit_v:
            vout_ref[...] = v
        if stride == 1:
            ri = lax.broadcasted_iota(jnp.int32, (CHUNK, CHUNK), 0)
            ci = lax.broadcasted_iota(jnp.int32, (CHUNK, CHUNK), 1)
            for hd in range(N_SG_HEADS):
                cols = slice(hd * SG_HEAD, (hd + 1) * SG_HEAD)
                wm = jnp.where(ri >= ci, ws_ref[hd], 0.0).astype(BF16)
                for c in range(rows // CHUNK):
                    r = slice(c * CHUNK, (c + 1) * CHUNK)
                    z = _dot(wm, v_scr[r, cols].astype(BF16)) + btab_ref[:, cols]
                    mix_scr[r, cols] = (u_scr[r, cols] * z).astype(BF16)
        else:
            for t in range(n_slabs):
                r = slice(t * slab, (t + 1) * slab)
                z = btab_ref[t:t + 1, :] + ws_ref[t, 0:1, :] * v_scr[0:slab, :]
                for s in range(1, t + 1):
                    z = z + ws_ref[t, s:s + 1, :] * v_scr[s * slab:(s + 1) * slab, :]
                mix_scr[r, :] = (u_scr[r, :] * z).astype(BF16)

    mix = _dot(mix_scr[...], wout_ref[...])
    for t in range(n_slabs):
        r = slice(t * slab, (t + 1) * slab)
        x1 = x_ref[r, :] + g1 * mix[r, :]
        x1_scr[r, :] = x1
        h2 = _rms(x1, nf_ref[...]) * (1.0 + sc2) + sh2
        h_scr[r, :] = h2.astype(BF16)

    dn = _conv_ffn(h_scr, x1_scr, g2, wup_ref, fcw_ref, fcb_ref, wdn_ref, fout_ref,
                   ug_scr, uv_scr, act_scr, rows=rows, stride=stride)
    for t in range(n_slabs):
        r = slice(t * slab, (t + 1) * slab)
        y = x1_scr[r, :] + g2 * dn[r, :]
        if final_norm:
            y = _rms(y, nfin_ref[...])
        y_ref[r, :] = y


def _resident(shape):
    zeros = (0,) * len(shape)
    return pl.BlockSpec(shape, lambda g, i: zeros, pipeline_mode=pl.Buffered(1))


def _per_group(shape, buffers=None):
    zeros = (0,) * (len(shape) - 1)
    mode = {} if buffers is None else {"pipeline_mode": pl.Buffered(buffers)}
    return pl.BlockSpec((None,) + tuple(shape[1:]), lambda g, i: (g,) + zeros, **mode)


def _layer_call(kind, x, mod, states, weights, *, n_slabs, slab, stride, pos0, final_norm, emit_v):
    groups, length, _ = x.shape
    rows = n_slabs * slab
    n_tiles = length // rows
    head_p = _round_up(POOL_BUF * stride, SUBLANES)
    head_c = _round_up(CONV_BUF * stride, SUBLANES)

    x_spec = pl.BlockSpec((None, rows, D_MODEL), lambda g, i: (g, i, 0))
    inputs = [x, mod] + list(states) + list(weights)
    in_specs = [x_spec, _per_group(mod.shape)]
    in_specs += [_per_group(s.shape, buffers=1) for s in states]
    in_specs += [_resident(w.shape) for w in weights]

    out_shape = [jax.ShapeDtypeStruct(x.shape, F32)]
    out_specs = [x_spec]
    for s in states:
        out_shape.append(jax.ShapeDtypeStruct(s.shape, F32))
        out_specs.append(_per_group(s.shape))
    if emit_v:
        out_shape.append(jax.ShapeDtypeStruct(x.shape, F32))
        out_specs.append(x_spec)

    scratch = [
        pltpu.VMEM((rows, D_MODEL), BF16),
        pltpu.VMEM((rows, D_MODEL), F32),
        pltpu.VMEM((rows, D_MODEL), BF16),
        pltpu.VMEM((head_c + rows, FF_CHUNK), F32),
        pltpu.VMEM((head_c + rows, FF_CHUNK), F32),
        pltpu.VMEM((rows, D_FF), BF16),
    ]
    if kind == "even":
        scratch += [
            pltpu.VMEM((head_p + rows, D_POOL), F32),
            pltpu.VMEM((head_c + rows, D_SCONV), F32),
            pltpu.VMEM((rows, D_SCONV), F32),
        ]
    else:
        scratch += [
            pltpu.VMEM((rows, D_MODEL), F32),
            pltpu.VMEM((rows, D_MODEL), F32),
        ]

    body = functools.partial(_layer_kernel, kind=kind, n_slabs=n_slabs, slab=slab, stride=stride,
                             pos0=pos0, final_norm=final_norm, emit_v=emit_v)
    return pl.pallas_call(
        body,
        grid=(groups, n_tiles),
        in_specs=in_specs,
        out_specs=out_specs,
        out_shape=out_shape,
        scratch_shapes=scratch,
        compiler_params=pltpu.CompilerParams(
            dimension_semantics=("arbitrary", "arbitrary"), vmem_limit_bytes=VMEM_LIMIT),
        name=f"{kind}_layer_s{stride}",
    )(*inputs)


def _to_time_major(a):
    b, t, c = a.shape
    a = a.reshape(b // SAMPLE_GROUP, SAMPLE_GROUP, t, c).transpose(0, 2, 1, 3)
    return a.reshape(b // SAMPLE_GROUP, t * SAMPLE_GROUP, c)


def _from_time_major(a, t):
    g, _, c = a.shape
    a = a.reshape(g, t, SAMPLE_GROUP, c).transpose(0, 2, 1, 3)
    return a.reshape(g * SAMPLE_GROUP, t, c)


def _front_pad_rows(a, rows):
    return jnp.pad(a, ((0, 0), (rows - a.shape[1], 0), (0, 0)))


def kernel(x_prompt, x_sample, state_pool, state_sconv, state_ffn, c_prompt, c_sample, norm_mix, norm_ffn, w_ada, b_ada, w_in0, w_pool, s_pool, sconv_w, sconv_b, w_out0, w_uv1, g_v1, w_s1, b_s1, w_out1, ffn_up, ffn_conv_w, ffn_conv_b, ffn_down, norm_final):
    batch, seq, _ = x_prompt.shape
    dec_batch, dec_seq, _ = x_sample.shape
    depth = norm_mix.shape[0]

    c_all = jnp.concatenate([c_prompt, c_sample], axis=0)
    c_rows = _round_up(c_all.shape[0], 16)
    c_all = jnp.pad(c_all, ((0, c_rows - c_all.shape[0]), (0, 0)))
    mod = _modulation(c_all, w_ada, b_ada)

    row = lambda v: v.reshape(1, -1)
    xp = x_prompt
    xs = _to_time_major(x_sample)
    n_sgroups = dec_batch // SAMPLE_GROUP

    outs = {k: [] for k in ("pool_p", "pool_s", "sconv_p", "sconv_s", "ffn_p", "ffn_s", "v_s")}
    for l in range(depth):
        last = l == depth - 1
        mod_p = mod[l, :batch].reshape(batch, 1, N_MOD * D_MODEL)
        mod_s = mod[l, batch:batch + dec_batch].reshape(n_sgroups, SAMPLE_GROUP, N_MOD * D_MODEL)
        ffn_w = [ffn_up[l].astype(BF16), ffn_conv_w[l], row(ffn_conv_b[l]), ffn_down[l].astype(BF16)]
        fin = [row(norm_final)] if last else []
        fst_p = jnp.zeros((batch, SUBLANES, 2 * D_FF), F32)
        fst_s = _to_time_major(state_ffn[l])
        if l % 2 == 0:
            e = l // 2
            w_common = [row(norm_mix[l]), row(norm_ffn[l]), w_in0[e].astype(BF16), w_pool[e].astype(BF16),
                        row(s_pool[e]), sconv_w[e], row(sconv_b[e]), w_out0[e].astype(BF16)] + ffn_w + fin
            st_p = [jnp.zeros((batch, 2 * SUBLANES, D_POOL), F32),
                    jnp.zeros((batch, SUBLANES, D_SCONV), F32), fst_p]
            st_s = [_to_time_major(state_pool[e]), _to_time_major(state_sconv[e]), fst_s]
            xp, pp, cp, fp = _layer_call("even", xp, mod_p, st_p, w_common, n_slabs=1, slab=TILE_ROWS,
                                         stride=1, pos0=0, final_norm=last, emit_v=False)
            xs, ps, cs, fs = _layer_call("even", xs, mod_s, st_s, w_common, n_slabs=dec_seq,
                                         slab=SAMPLE_GROUP, stride=SAMPLE_GROUP, pos0=PAST_LEN,
                                         final_norm=last, emit_v=False)
            outs["pool_p"].append(pp[:, 2 * SUBLANES - POOL_BUF:])
            outs["sconv_p"].append(cp[:, SUBLANES - CONV_BUF:])
            outs["pool_s"].append(_from_time_major(ps, POOL_BUF))
            outs["sconv_s"].append(_from_time_major(cs, CONV_BUF))
        else:
            o = l // 2
            b_tab = jnp.repeat(b_s1[o].T, SG_HEAD, axis=1)
            ws_small = jnp.repeat(w_s1[o][:, :dec_seq, :dec_seq].transpose(1, 2, 0), SG_HEAD, axis=2)
            head = [row(norm_mix[l]), row(norm_ffn[l]), w_uv1[o].astype(BF16), row(g_v1[o])]
            tail = [b_tab, w_out1[o].astype(BF16)] + ffn_w + fin
            xp, fp = _layer_call("odd", xp, mod_p, [fst_p], head + [w_s1[o]] + tail, n_slabs=1,
                                 slab=TILE_ROWS, stride=1, pos0=0, final_norm=last, emit_v=False)
            xs, fs, vs = _layer_call("odd", xs, mod_s, [fst_s], head + [ws_small] + tail,
                                     n_slabs=dec_seq, slab=SAMPLE_GROUP, stride=SAMPLE_GROUP,
                                     pos0=PAST_LEN, final_norm=last, emit_v=True)
            outs["v_s"].append(_from_time_major(vs, dec_seq))
        outs["ffn_p"].append(fp[:, SUBLANES - CONV_BUF:])
        outs["ffn_s"].append(_from_time_major(fs, CONV_BUF))

    y_prompt = xp
    y_sample = _from_time_major(xs, dec_seq)
    st = lambda k: jnp.stack(outs[k])
    return (y_prompt, y_sample, st("pool_p"), st("pool_s"), st("sconv_p"), st("sconv_s"),
            st("ffn_p"), st("ffn_s"), st("v_s"))
```

```python
import functools
import math

import jax
import jax.numpy as jnp
from jax import lax
from jax.experimental import pallas as pl
from jax.experimental.pallas import tpu as pltpu

D_MODEL = 1024
PAST_LEN = 16384
D_POOL = 512
D_SCONV = 512
POOL_WINDOWS = (2, 4, 8, 16)
POOL_GROUP = 128
POOL_BUF = 15
CONV_BUF = 2
CHUNK = 128
N_SG_HEADS = 8
SG_HEAD = 128
D_FF = 2816
N_MOD = 6
EPS = 1e-6

SUBLANES = 8
TILE_ROWS = 512
FF_CHUNK = 256
N_FF_CHUNKS = D_FF // FF_CHUNK
SAMPLE_GROUP = 64
MOD_TILE_N = 1536
ROW_BLOCK = 64
NORM_BLOCK = 32
SG_BLOCK = 16
VMEM_LIMIT = 60 * 1024 * 1024

F32 = jnp.float32
BF16 = jnp.bfloat16

_GELU_C0 = math.sqrt(2.0 / math.pi)
_GELU_C1 = 0.044715 * _GELU_C0


def _round_up(n, m):
    return -(-n // m) * m


def _gelu(x):
    return 0.5 * x * (1.0 + jnp.tanh(x * (_GELU_C0 + _GELU_C1 * (x * x))))


def _dot(a, b):
    return jnp.dot(a, b, preferred_element_type=F32)


def _rms(x, g):
    ms = jnp.mean(x * x, axis=-1, keepdims=True)
    return x * lax.rsqrt(ms + EPS) * g


def _mod_kernel(c_ref, w_ref, b_ref, o_ref):
    c = c_ref[...]
    s = (c * jax.nn.sigmoid(c)).astype(BF16)
    o_ref[...] = _dot(s, w_ref[...].astype(BF16)) + b_ref[...]


def _modulation(c_all, w_ada, b_ada):
    depth, _, n = w_ada.shape
    rows = c_all.shape[0]
    return pl.pallas_call(
        _mod_kernel,
        grid=(depth, n // MOD_TILE_N),
        in_specs=[
            pl.BlockSpec((rows, D_MODEL), lambda l, j: (0, 0)),
            pl.BlockSpec((None, D_MODEL, MOD_TILE_N), lambda l, j: (l, 0, j)),
            pl.BlockSpec((None, 1, MOD_TILE_N), lambda l, j: (l, 0, j)),
        ],
        out_specs=pl.BlockSpec((None, rows, MOD_TILE_N), lambda l, j: (l, 0, j)),
        out_shape=jax.ShapeDtypeStruct((depth, rows, n), F32),
        compiler_params=pltpu.CompilerParams(
            dimension_semantics=("arbitrary", "arbitrary"), vmem_limit_bytes=VMEM_LIMIT),
        name="adaln_mod",
    )(c_all, w_ada, b_ada.reshape(depth, 1, n))


def _blocks(rows, size):
    return [slice(o, o + size) for o in range(0, rows, size)]


def _mod_rows(mod_ref, part, r, slab):
    cols = slice(part * D_MODEL, (part + 1) * D_MODEL)
    if mod_ref.shape[0] == 1:
        return mod_ref[:, cols]
    o = r.start % slab
    return mod_ref[o:o + (r.stop - r.start), cols]


def _shift_rows(cur, prev8, k):
    if k == SUBLANES:
        return jnp.concatenate([prev8, cur[:-SUBLANES, :]], axis=0)
    return pltpu.roll(jnp.concatenate([prev8, cur], axis=0), k, axis=0)[SUBLANES:, :]


def _conv3_blocks(ext, head, stride, w_ref, b_ref, cols):
    rows = ext.shape[0] - head
    w0, w1, w2, bias = w_ref[0:1, cols], w_ref[1:2, cols], w_ref[2:3, cols], b_ref[:, cols]
    if stride == 1:
        hdr = ext[head - SUBLANES:head, :]
        q_prev = w0 * hdr
        r_prev = w1 * hdr + pltpu.roll(q_prev, 1, axis=0)
    for r in _blocks(rows, ROW_BLOCK):
        e = ext[head + r.start:head + r.stop, :]
        if stride == 1:
            q = w0 * e
            p = w1 * e + _shift_rows(q, q_prev, 1)
            yield (w2 * e + bias) + _shift_rows(p, r_prev, 1)
            q_prev, r_prev = q[-SUBLANES:, :], p[-SUBLANES:, :]
        else:
            y = bias + w0 * ext[head - 2 * stride + r.start:head - 2 * stride + r.stop, :]
            y = y + w1 * ext[head - stride + r.start:head - stride + r.stop, :]
            yield y + w2 * e


def _window_sum_blocks(ext, head, stride):
    rows = ext.shape[0] - head
    groups = [slice(g * POOL_GROUP, (g + 1) * POOL_GROUP) for g in range(len(POOL_WINDOWS))]
    if stride != 1:
        for r in _blocks(rows, ROW_BLOCK):
            out = []
            for cols, w in zip(groups, POOL_WINDOWS):
                s = ext[head + r.start:head + r.stop, cols]
                for j in range(1, w):
                    s = s + ext[head - j * stride + r.start:head - j * stride + r.stop, cols]
                out.append(s)
            yield out
        return
    carry = {}

    def level_sums(e, cols_id, w):
        s, k = e, 1
        while k < w:
            prev = carry.get((cols_id, k), jnp.zeros((SUBLANES, POOL_GROUP), F32))
            carry[(cols_id, k)] = s[-SUBLANES:, :]
            s = s + _shift_rows(s, prev, k)
            k *= 2
        return s

    for g, (cols, w) in enumerate(zip(groups, POOL_WINDOWS)):
        level_sums(ext[0:head, cols], g, w)
    for r in _blocks(rows, ROW_BLOCK):
        yield [level_sums(ext[head + r.start:head + r.stop, cols], g, w)
               for g, (cols, w) in enumerate(zip(groups, POOL_WINDOWS))]


def _conv_ffn(h_scr, wup_ref, fcw_ref, fcb_ref, wdn_ref, fout_ref, act_scr, *, rows, stride):
    head = fout_ref.shape[0]
    for j in range(N_FF_CHUNKS):
        gc = slice(j * FF_CHUNK, (j + 1) * FF_CHUNK)
        vc = slice(D_FF + j * FF_CHUNK, D_FF + (j + 1) * FF_CHUNK)
        ug = jnp.concatenate([fout_ref[:, gc], _dot(h_scr[...], wup_ref[:, gc])], axis=0)
        uv = jnp.concatenate([fout_ref[:, vc], _dot(h_scr[...], wup_ref[:, vc])], axis=0)
        fout_ref[:, gc] = ug[rows:, :]
        fout_ref[:, vc] = uv[rows:, :]
        ga_blocks = _conv3_blocks(ug, head, stride, fcw_ref, fcb_ref, gc)
        va_blocks = _conv3_blocks(uv, head, stride, fcw_ref, fcb_ref, vc)
        for r, ga, va in zip(_blocks(rows, ROW_BLOCK), ga_blocks, va_blocks):
            act_scr[r, gc] = (_gelu(ga) * va).astype(BF16)
    return _dot(act_scr[...], wdn_ref[...])


def _layer_kernel(*refs, kind, n_slabs, slab, stride, pos0, final_norm, emit_v):
    rows = n_slabs * slab
    it = iter(refs)
    x_ref, mod_ref = next(it), next(it)
    if kind == "even":
        pst_ref, cst_ref = next(it), next(it)
    fst_ref = next(it)
    nm_ref, nf_ref = next(it), next(it)
    if kind == "even":
        win_ref, wpool_ref, spool_ref, scw_ref, scb_ref, wout_ref = (next(it) for _ in range(6))
    else:
        wuv_ref, gv_ref, ws_ref, btab_ref, wout_ref = (next(it) for _ in range(5))
    wup_ref, fcw_ref, fcb_ref, wdn_ref = (next(it) for _ in range(4))
    nfin_ref = next(it) if final_norm else None
    y_ref = next(it)
    if kind == "even":
        pout_ref, cout_ref = next(it), next(it)
    fout_ref = next(it)
    vout_ref = next(it) if emit_v else None
    h_scr, x1_scr, mix_scr, act_scr = (next(it) for _ in range(4))
    if kind == "even":
        d_scr = next(it)
    else:
        u_scr, v_scr = next(it), next(it)

    i = pl.program_id(1)
    SH1, SC1, G1, SH2, SC2, G2 = range(N_MOD)
    mod = lambda part, r: _mod_rows(mod_ref, part, r, slab)

    @pl.when(i == 0)
    def _():
        fout_ref[...] = fst_ref[...]
        if kind == "even":
            pout_ref[...] = pst_ref[...]
            cout_ref[...] = cst_ref[...]

    for r in _blocks(rows, NORM_BLOCK):
        h = _rms(x_ref[r, :], nm_ref[...]) * (1.0 + mod(SC1, r)) + mod(SH1, r)
        h_scr[r, :] = h.astype(BF16)

    if kind == "even":
        hp = pout_ref.shape[0]
        hc = cout_ref.shape[0]
        a = _dot(h_scr[...], win_ref[:, 0:D_POOL])
        xin = _dot(h_scr[...], win_ref[:, D_POOL:D_POOL + D_SCONV])
        cg = _dot(h_scr[...], win_ref[:, D_POOL + 2 * D_SCONV:D_POOL + 3 * D_SCONV])
        bg = _dot(h_scr[...], win_ref[:, D_POOL + D_SCONV:D_POOL + 2 * D_SCONV])

        pext = jnp.concatenate([pout_ref[...], a], axis=0)
        pout_ref[...] = pext[rows:, :]
        for r, sums in zip(_blocks(rows, ROW_BLOCK), _window_sum_blocks(pext, hp, stride)):
            ridx = lax.broadcasted_iota(jnp.int32, (ROW_BLOCK, POOL_GROUP), 0)
            pos = pos0 + (i * rows + r.start + ridx) // stride
            for g, (w, s) in enumerate(zip(POOL_WINDOWS, sums)):
                cols = slice(g * POOL_GROUP, (g + 1) * POOL_GROUP)
                cnt = jnp.minimum(w, pos + 1).astype(F32)
                d_scr[r, cols] = (s / cnt - a[r, cols]).astype(BF16)
        for g in range(len(POOL_WINDOWS)):
            cols = slice(g * POOL_GROUP, (g + 1) * POOL_GROUP)
            ya = _dot(d_scr[:, cols], wpool_ref[g])
            for r in _blocks(rows, ROW_BLOCK):
                mix_scr[r, cols] = (ya[r, :] * spool_ref[:, cols]).astype(BF16)

        zext = jnp.concatenate([cout_ref[...], cg * xin], axis=0)
        cout_ref[...] = zext[rows:, :]
        yc_blocks = _conv3_blocks(zext, hc, stride, scw_ref, scb_ref, slice(0, D_SCONV))
        for r, yc in zip(_blocks(rows, ROW_BLOCK), yc_blocks):
            mix_scr[r, D_POOL:D_POOL + D_SCONV] = (bg[r, :] * yc).astype(BF16)
    else:
        u = _dot(h_scr[...], wuv_ref[:, 0:D_MODEL])
        v = _dot(h_scr[...], wuv_ref[:, D_MODEL:2 * D_MODEL])
        for r in _blocks(rows, NORM_BLOCK):
            u_scr[r, :] = _gelu(u[r, :])
            vn = _rms(_gelu(v[r, :]), gv_ref[...])
            v_scr[r, :] = vn
            if emit_v:
                vout_ref[r, :] = vn
        if stride == 1:
            ri = lax.broadcasted_iota(jnp.int32, (CHUNK, CHUNK), 0)
            ci = lax.broadcasted_iota(jnp.int32, (CHUNK, CHUNK), 1)
            for hd in range(N_SG_HEADS):
                cols = slice(hd * SG_HEAD, (hd + 1) * SG_HEAD)
                wm = jnp.where(ri >= ci, ws_ref[hd], 0.0).astype(BF16)
                for r in _blocks(rows, CHUNK):
                    z = _dot(wm, v_scr[r, cols].astype(BF16)) + btab_ref[:, cols]
                    mix_scr[r, cols] = (u_scr[r, cols] * z).astype(BF16)
        else:
            for r in _blocks(rows, SG_BLOCK):
                t, o = divmod(r.start, slab)
                z = btab_ref[t:t + 1, :] + ws_ref[t, 0:1, :] * v_scr[o:o + SG_BLOCK, :]
                for s in range(1, t + 1):
                    z = z + ws_ref[t, s:s + 1, :] * v_scr[s * slab + o:s * slab + o + SG_BLOCK, :]
                mix_scr[r, :] = (u_scr[r, :] * z).astype(BF16)

    mix = _dot(mix_scr[...], wout_ref[...])
    for r in _blocks(rows, NORM_BLOCK):
        x1 = x_ref[r, :] + mod(G1, r) * mix[r, :]
        x1_scr[r, :] = x1
        h2 = _rms(x1, nf_ref[...]) * (1.0 + mod(SC2, r)) + mod(SH2, r)
        h_scr[r, :] = h2.astype(BF16)

    dn = _conv_ffn(h_scr, wup_ref, fcw_ref, fcb_ref, wdn_ref, fout_ref, act_scr,
                   rows=rows, stride=stride)
    for r in _blocks(rows, NORM_BLOCK):
        y = x1_scr[r, :] + mod(G2, r) * dn[r, :]
        if final_norm:
            y = _rms(y, nfin_ref[...])
        y_ref[r, :] = y


def _resident(shape):
    zeros = (0,) * len(shape)
    return pl.BlockSpec(shape, lambda g, i: zeros, pipeline_mode=pl.Buffered(1))


def _per_group(shape, buffers=None):
    zeros = (0,) * (len(shape) - 1)
    mode = {} if buffers is None else {"pipeline_mode": pl.Buffered(buffers)}
    return pl.BlockSpec((None,) + tuple(shape[1:]), lambda g, i: (g,) + zeros, **mode)


def _layer_call(kind, x, mod, states, weights, *, n_slabs, slab, stride, pos0, final_norm, emit_v):
    groups, length, _ = x.shape
    rows = n_slabs * slab
    n_tiles = length // rows

    x_spec = pl.BlockSpec((None, rows, D_MODEL), lambda g, i: (g, i, 0))
    inputs = [x, mod] + list(states) + list(weights)
    in_specs = [x_spec, _per_group(mod.shape)]
    in_specs += [_per_group(s.shape, buffers=1) for s in states]
    in_specs += [_resident(w.shape) for w in weights]

    out_shape = [jax.ShapeDtypeStruct(x.shape, F32)]
    out_specs = [x_spec]
    for s in states:
        out_shape.append(jax.ShapeDtypeStruct(s.shape, F32))
        out_specs.append(_per_group(s.shape))
    if emit_v:
        out_shape.append(jax.ShapeDtypeStruct(x.shape, F32))
        out_specs.append(x_spec)

    scratch = [
        pltpu.VMEM((rows, D_MODEL), BF16),
        pltpu.VMEM((rows, D_MODEL), F32),
        pltpu.VMEM((rows, D_MODEL), BF16),
        pltpu.VMEM((rows, D_FF), BF16),
    ]
    if kind == "even":
        scratch += [pltpu.VMEM((rows, D_POOL), BF16)]
    else:
        scratch += [
            pltpu.VMEM((rows, D_MODEL), F32),
            pltpu.VMEM((rows, D_MODEL), F32),
        ]

    body = functools.partial(_layer_kernel, kind=kind, n_slabs=n_slabs, slab=slab, stride=stride,
                             pos0=pos0, final_norm=final_norm, emit_v=emit_v)
    return pl.pallas_call(
        body,
        grid=(groups, n_tiles),
        in_specs=in_specs,
        out_specs=out_specs,
        out_shape=out_shape,
        scratch_shapes=scratch,
        compiler_params=pltpu.CompilerParams(
            dimension_semantics=("arbitrary", "arbitrary"), vmem_limit_bytes=VMEM_LIMIT),
        name=f"{kind}_layer_s{stride}",
    )(*inputs)


def _to_time_major(a):
    b, t, c = a.shape
    a = a.reshape(b // SAMPLE_GROUP, SAMPLE_GROUP, t, c).transpose(0, 2, 1, 3)
    return a.reshape(b // SAMPLE_GROUP, t * SAMPLE_GROUP, c)


def _from_time_major(a, t):
    g, _, c = a.shape
    a = a.reshape(g, t, SAMPLE_GROUP, c).transpose(0, 2, 1, 3)
    return a.reshape(g * SAMPLE_GROUP, t, c)


def kernel(x_prompt, x_sample, state_pool, state_sconv, state_ffn, c_prompt, c_sample, norm_mix, norm_ffn, w_ada, b_ada, w_in0, w_pool, s_pool, sconv_w, sconv_b, w_out0, w_uv1, g_v1, w_s1, b_s1, w_out1, ffn_up, ffn_conv_w, ffn_conv_b, ffn_down, norm_final):
    batch, seq, _ = x_prompt.shape
    dec_batch, dec_seq, _ = x_sample.shape
    depth = norm_mix.shape[0]

    c_all = jnp.concatenate([c_prompt, c_sample], axis=0)
    c_rows = _round_up(c_all.shape[0], 16)
    c_all = jnp.pad(c_all, ((0, c_rows - c_all.shape[0]), (0, 0)))
    mod = _modulation(c_all, w_ada, b_ada)

    row = lambda v: v.reshape(1, -1)
    xp = x_prompt
    xs = _to_time_major(x_sample)
    n_sgroups = dec_batch // SAMPLE_GROUP

    outs = {k: [] for k in ("pool_p", "pool_s", "sconv_p", "sconv_s", "ffn_p", "ffn_s", "v_s")}
    for l in range(depth):
        last = l == depth - 1
        mod_p = mod[l, :batch].reshape(batch, 1, N_MOD * D_MODEL)
        mod_s = mod[l, batch:batch + dec_batch].reshape(n_sgroups, SAMPLE_GROUP, N_MOD * D_MODEL)
        ffn_w = [ffn_up[l].astype(BF16), ffn_conv_w[l], row(ffn_conv_b[l]), ffn_down[l].astype(BF16)]
        fin = [row(norm_final)] if last else []
        fst_p = jnp.zeros((batch, SUBLANES, 2 * D_FF), F32)
        fst_s = _to_time_major(state_ffn[l])
        if l % 2 == 0:
            e = l // 2
            w_common = [row(norm_mix[l]), row(norm_ffn[l]), w_in0[e].astype(BF16), w_pool[e].astype(BF16),
                        row(s_pool[e]), sconv_w[e], row(sconv_b[e]), w_out0[e].astype(BF16)] + ffn_w + fin
            st_p = [jnp.zeros((batch, 2 * SUBLANES, D_POOL), F32),
                    jnp.zeros((batch, SUBLANES, D_SCONV), F32), fst_p]
            st_s = [_to_time_major(state_pool[e]), _to_time_major(state_sconv[e]), fst_s]
            xp, pp, cp, fp = _layer_call("even", xp, mod_p, st_p, w_common, n_slabs=1, slab=TILE_ROWS,
                                         stride=1, pos0=0, final_norm=last, emit_v=False)
            xs, ps, cs, fs = _layer_call("even", xs, mod_s, st_s, w_common, n_slabs=dec_seq,
                                         slab=SAMPLE_GROUP, stride=SAMPLE_GROUP, pos0=PAST_LEN,
                                         final_norm=last, emit_v=False)
            outs["pool_p"].append(pp[:, 2 * SUBLANES - POOL_BUF:])
            outs["sconv_p"].append(cp[:, SUBLANES - CONV_BUF:])
            outs["pool_s"].append(_from_time_major(ps, POOL_BUF))
            outs["sconv_s"].append(_from_time_major(cs, CONV_BUF))
        else:
            o = l // 2
            b_tab = jnp.repeat(b_s1[o].T, SG_HEAD, axis=1)
            ws_small = jnp.repeat(w_s1[o][:, :dec_seq, :dec_seq].transpose(1, 2, 0), SG_HEAD, axis=2)
            head = [row(norm_mix[l]), row(norm_ffn[l]), w_uv1[o].astype(BF16), row(g_v1[o])]
            tail = [b_tab, w_out1[o].astype(BF16)] + ffn_w + fin
            xp, fp = _layer_call("odd", xp, mod_p, [fst_p], head + [w_s1[o]] + tail, n_slabs=1,
                                 slab=TILE_ROWS, stride=1, pos0=0, final_norm=last, emit_v=False)
            xs, fs, vs = _layer_call("odd", xs, mod_s, [fst_s], head + [ws_small] + tail,
                                     n_slabs=dec_seq, slab=SAMPLE_GROUP, stride=SAMPLE_GROUP,
                                     pos0=PAST_LEN, final_norm=last, emit_v=True)
            outs["v_s"].append(_from_time_major(vs, dec_seq))
        outs["ffn_p"].append(fp[:, SUBLANES - CONV_BUF:])
        outs["ffn_s"].append(_from_time_major(fs, CONV_BUF))

    y_prompt = xp
    y_sample = _from_time_major(xs, dec_seq)
    st = lambda k: jnp.stack(outs[k])
    return (y_prompt, y_sample, st("pool_p"), st("pool_s"), st("sconv_p"), st("sconv_s"),
            st("ffn_p"), st("ffn_s"), st("v_s"))
```
